```python
import math
import jax, jax.numpy as jnp
from jax import lax
import numpy as np

D_MODEL = 1024
BATCH = 16
SEQ = 256
DEPTH = 4
DEC_BATCH = 8
DEC_SEQ = 1024
PAST_LEN = 512

GRID_W = 64
N_BRANCH = 3
MIX_WIDTH = D_MODEL
D_FF = 2816
SSD_HEAD_DIM = 64
SSD_HEADS = MIX_WIDTH // SSD_HEAD_DIM
SSD_GROUPS = 4
SSD_STATE = 64
SSD_CONV = 5
SSD_CHUNK = 128
SSD_CONV_CH = MIX_WIDTH + 2 * SSD_GROUPS * SSD_STATE
DIFF_HEAD_DIM = 64
DIFF_HEADS = MIX_WIDTH // (2 * DIFF_HEAD_DIM)
DIFF_QK_WIDTH = DIFF_HEADS * 2 * DIFF_HEAD_DIM
DIFF_V_WIDTH = DIFF_HEADS * 2 * DIFF_HEAD_DIM
Q_BLOCK = 128
POOL_WINDOWS = (2, 4, 8, 16)
POOL_GROUPS = 4
POOL_GROUP_WIDTH = MIX_WIDTH // POOL_GROUPS
IN_WIDTH = MIX_WIDTH + SSD_CONV_CH + 2 * SSD_HEADS + 2 * DIFF_QK_WIDTH + DIFF_V_WIDTH + MIX_WIDTH + N_BRANCH * D_MODEL
N_MOD = 9
ROPE_THETA = 10000.0
RMS_EPS = 1e-6

kernel_name = "hybrid_diffusion_ssd_diffattn_pool_step"


def rmsnorm(x, gain):
    xf = x.astype(jnp.float32)
    y = xf * lax.rsqrt(jnp.mean(xf * xf, axis=-1, keepdims=True) + RMS_EPS)
    return (y * gain.astype(jnp.float32)).astype(x.dtype)


def modulate(h, shift, scale):
    return h * (1 + scale) + shift


def swiglu(h, w_in, w_out):
    g, u = jnp.split(h @ w_in, 2, axis=-1)
    return (jax.nn.silu(g) * u) @ w_out


def split_columns(a, sizes):
    points = []
    acc = 0
    for s in sizes[:-1]:
        acc += s
        points.append(acc)
    return jnp.split(a, points, axis=-1)


def depthwise_conv(u, w, bias):
    taps, ch = w.shape
    out = lax.conv_general_dilated(u, w[:, None, :].astype(u.dtype), window_strides=(1,),
                                   padding=[(taps // 2, taps // 2)],
                                   dimension_numbers=("NWC", "WIO", "NWC"),
                                   feature_group_count=ch)
    return out + bias


def ssd_chunked(x, dt, a_coef, bm, cm, h0):
    b, L, H, P = x.shape
    nc = L // SSD_CHUNK
    rep = H // bm.shape[2]
    dtype = x.dtype
    bh = jnp.repeat(bm, rep, axis=2).reshape(b, nc, SSD_CHUNK, H, -1)
    ch = jnp.repeat(cm, rep, axis=2).reshape(b, nc, SSD_CHUNK, H, -1)
    xdt = (x * dt[..., None].astype(dtype)).reshape(b, nc, SSD_CHUNK, H, P)
    a = (dt * a_coef).reshape(b, nc, SSD_CHUNK, H).transpose(0, 1, 3, 2)
    a_cum = jnp.cumsum(a, axis=-1)
    causal = jnp.tril(jnp.ones((SSD_CHUNK, SSD_CHUNK), dtype=bool))
    seg = a_cum[..., :, None] - a_cum[..., None, :]
    decay_in = jnp.exp(jnp.where(causal, seg, -jnp.inf)).astype(dtype)
    scores = jnp.einsum('bclhn,bcshn->bchls', ch, bh) * decay_in
    y_diag = jnp.einsum('bchls,bcshp->bclhp', scores, xdt)
    decay_to_end = jnp.exp(a_cum[..., -1:] - a_cum).astype(dtype)
    chunk_states = jnp.einsum('bclhn,bchl,bclhp->bchpn', bh, decay_to_end, xdt)
    chunk_decay = jnp.exp(a_cum[..., -1]).astype(dtype)

    def step(h, inp):
        st, dec = inp
        return h * dec[..., None, None] + st, h

    h_final, h_prev = lax.scan(step, h0.astype(dtype),
                               (chunk_states.swapaxes(0, 1), chunk_decay.swapaxes(0, 1)))
    h_prev = h_prev.swapaxes(0, 1)
    decay_from_start = jnp.exp(a_cum).astype(dtype)
    y_off = jnp.einsum('bclhn,bchpn,bchl->bclhp', ch, h_prev, decay_from_start)
    return (y_diag + y_off).reshape(b, L, H, P), h_final


def axial_rope(t, rows):
    pos_row = jnp.repeat(jnp.arange(rows), GRID_W)
    pos_col = jnp.tile(jnp.arange(GRID_W), rows)
    n_freq = DIFF_HEAD_DIM // 4
    inv_freq = ROPE_THETA ** (-jnp.arange(n_freq, dtype=jnp.float32) / n_freq)

    def rotate(u, pos):
        ang = pos.astype(jnp.float32)[:, None] * inv_freq
        cos = jnp.cos(ang)[None, :, None, None, :].astype(u.dtype)
        sin = jnp.sin(ang)[None, :, None, None, :].astype(u.dtype)
        u1, u2 = u[..., :n_freq], u[..., n_freq:]
        return jnp.concatenate([u1 * cos - u2 * sin, u1 * sin + u2 * cos], axis=-1)

    half = DIFF_HEAD_DIM // 2
    return jnp.concatenate([rotate(t[..., :half], pos_row), rotate(t[..., half:], pos_col)], axis=-1)


def diff_attention(q, k, v, lam):
    b, lq, H, _, d = q.shape
    nb = lq // Q_BLOCK
    qb = q.reshape(b, nb, Q_BLOCK, H, 2, d).swapaxes(0, 1)

    def attend(q_blk):
        s = jnp.einsum('bqhcd,bkhcd->bhcqk', q_blk, k).astype(jnp.float32) * (d ** -0.5)
        p = jax.nn.softmax(s, axis=-1)
        a = p[:, :, 0] - lam * p[:, :, 1]
        return jnp.einsum('bhqk,bkhe->bqhe', a.astype(v.dtype), v)

    o = lax.map(attend, qb)
    return o.swapaxes(0, 1).reshape(b, lq, H, -1)


def multiscale_pool(u, w_map, scale):
    b, L, _ = u.shape
    ug = u.reshape(b, L, POOL_GROUPS, POOL_GROUP_WIDTH)
    cs = lax.cumsum(ug.astype(jnp.float32), axis=1)
    cs = jnp.pad(cs, ((0, 0), (1, 0), (0, 0), (0, 0)))
    win = jnp.array(POOL_WINDOWS, dtype=jnp.int32)
    t = jnp.arange(L, dtype=jnp.int32)[:, None]
    lo = jnp.clip(t - win // 2, 0, L)
    hi = jnp.clip(t + win - win // 2, 0, L)
    g = jnp.arange(POOL_GROUPS)
    window_sum = cs[:, hi, g] - cs[:, lo, g]
    count = (hi - lo).astype(jnp.float32)[None, :, :, None]
    pooled = (window_sum / count).astype(u.dtype) - ug
    mixed = jnp.einsum('blgc,gce->blge', pooled, w_map)
    return mixed.reshape(b, L, MIX_WIDTH) * scale


def token_mixer(h, l, p, ctx):
    b, L, _ = h.shape
    proj = h @ p["w_in"][l]
    sizes = (MIX_WIDTH, SSD_CONV_CH, 2 * SSD_HEADS, DIFF_QK_WIDTH, DIFF_QK_WIDTH, DIFF_V_WIDTH,
             MIX_WIDTH, N_BRANCH * D_MODEL)
    z, xbc, dt_raw, q, k, v, u_pool, gate_logits = split_columns(proj, sizes)

    xbc = jax.nn.silu(depthwise_conv(xbc, p["ssd_conv_w"][l], p["ssd_conv_b"][l]))
    xs, bm, cm = split_columns(xbc, (MIX_WIDTH, SSD_GROUPS * SSD_STATE, SSD_GROUPS * SSD_STATE))
    xs = xs.reshape(b, L, SSD_HEADS, SSD_HEAD_DIM)
    bm = bm.reshape(b, L, SSD_GROUPS, SSD_STATE)
    cm = cm.reshape(b, L, SSD_GROUPS, SSD_STATE)
    dt = jax.nn.softplus(dt_raw.reshape(b, L, 2, SSD_HEADS).astype(jnp.float32)
                         + p["ssd_dt_bias"][l].astype(jnp.float32))
    a_coef = -jnp.exp(p["ssd_a_log"][l].astype(jnp.float32))
    if ctx is None:
        h0 = jnp.zeros((b, 2, SSD_HEADS, SSD_HEAD_DIM, SSD_STATE), h.dtype)
    else:
        h0 = ctx[2]
    y_fwd, hf_fwd = ssd_chunked(xs, dt[:, :, 0], a_coef[0], bm, cm, h0[:, 0])
    y_bwd, hf_bwd = ssd_chunked(xs[:, ::-1], dt[:, ::-1, 1], a_coef[1], bm[:, ::-1], cm[:, ::-1], h0[:, 1])
    y = y_fwd + y_bwd[:, ::-1] + p["ssd_d"][l][:, None] * xs
    y_ssd = rmsnorm(y.reshape(b, L, MIX_WIDTH) * jax.nn.silu(z), p["ssd_norm_gain"][l])

    q = q.reshape(b, L, DIFF_HEADS, 2, DIFF_HEAD_DIM)
    k = k.reshape(b, L, DIFF_HEADS, 2, DIFF_HEAD_DIM)
    v = v.reshape(b, L, DIFF_HEADS, 2 * DIFF_HEAD_DIM)
    lam_init = 0.8 - 0.6 * math.exp(-0.3 * l)
    lq1, lk1, lq2, lk2 = p["diff_lambda"][l].astype(jnp.float32)
    lam = jnp.exp(jnp.sum(lq1 * lk1)) - jnp.exp(jnp.sum(lq2 * lk2)) + lam_init
    if ctx is None:
        k_all, v_all = k, v
    else:
        rows = L // GRID_W
        q = axial_rope(q, rows)
        k = axial_rope(k, rows)
        k_ctx = ctx[0].reshape(b, -1, DIFF_HEADS, 2, DIFF_HEAD_DIM)
        k_all = jnp.concatenate([k_ctx, k], axis=1)
        v_all = jnp.concatenate([ctx[1], v], axis=1)
    o = diff_attention(q, k_all, v_all, lam)
    o_diff = (rmsnorm(o, p["diff_norm_gain"][l]) * (1 - lam_init)).reshape(b, L, MIX_WIDTH)

    y_pool = multiscale_pool(u_pool, p["pool_map"][l], p["pool_scale"][l])

    branches = jnp.stack([y_ssd, o_diff, y_pool], axis=2)
    proj_b = jnp.einsum('blnw,nwd->blnd', branches, p["w_branch"][l])
    gates = jax.nn.sigmoid(gate_logits.reshape(b, L, N_BRANCH, D_MODEL))
    merged = jnp.sum(gates * proj_b, axis=2)
    out = merged @ p["w_out"][l]
    if ctx is None:
        ctx_out = (k.reshape(b, L, DIFF_HEADS, 2 * DIFF_HEAD_DIM), v, jnp.stack([hf_fwd, hf_bwd], axis=1))
    else:
        ctx_out = None
    return out, ctx_out


def trunk_layer(x, cond, l, p, ctx):
    mod = jax.nn.silu(cond) @ p["w_ada"][l] + p["b_ada"][l]
    mod = mod.reshape(mod.shape[0], 1, N_MOD, D_MODEL)
    g = p["norm_gain"][l]
    h = modulate(rmsnorm(x, g[0]), mod[:, :, 0], mod[:, :, 1])
    x = x + 0.5 * mod[:, :, 2] * swiglu(h, p["ffn_w_in"][l, 0], p["ffn_w_out"][l, 0])
    h = modulate(rmsnorm(x, g[1]), mod[:, :, 3], mod[:, :, 4])
    mix, ctx_out = token_mixer(h, l, p, ctx)
    x = x + mod[:, :, 5] * mix
    h = modulate(rmsnorm(x, g[2]), mod[:, :, 6], mod[:, :, 7])
    x = x + 0.5 * mod[:, :, 8] * swiglu(h, p["ffn_w_in"][l, 1], p["ffn_w_out"][l, 1])
    return x, ctx_out


def setup_inputs(seed: int = 0) -> dict:
    key = jax.random.key(seed)
    ks = jax.random.split(key, 26)
    f32 = jnp.float32

    def nrm(k, shape, scale):
        return jax.random.normal(k, shape, f32) * scale

    dt0 = jnp.exp(jax.random.uniform(ks[12], (DEPTH, 2, SSD_HEADS), f32, math.log(1e-3), math.log(1e-1)))
    return {
        "x_prompt": nrm(ks[0], (BATCH, SEQ, D_MODEL), 1.0),
        "x_sample": nrm(ks[1], (DEC_BATCH, DEC_SEQ, D_MODEL), 1.0),
        "cache_k": nrm(ks[2], (DEC_BATCH, DEPTH, PAST_LEN, DIFF_HEADS, 2 * DIFF_HEAD_DIM), 1.0),
        "cache_v": nrm(ks[3], (DEC_BATCH, DEPTH, PAST_LEN, DIFF_HEADS, 2 * DIFF_HEAD_DIM), 1.0),
        "state_ssm": nrm(ks[4], (DEC_BATCH, DEPTH, 2, SSD_HEADS, SSD_HEAD_DIM, SSD_STATE), 0.1),
        "c": nrm(ks[5], (DEC_BATCH, D_MODEL), 1.0),
        "c_ctx": nrm(ks[6], (D_MODEL,), 1.0),
        "w_ada": nrm(ks[7], (DEPTH, D_MODEL, N_MOD * D_MODEL), 0.5 * D_MODEL ** -0.5),
        "b_ada": nrm(ks[8], (DEPTH, N_MOD * D_MODEL), 0.02),
        "norm_gain": 1.0 + nrm(ks[9], (DEPTH, 3, D_MODEL), 0.05),
        "ffn_w_in": nrm(ks[10], (DEPTH, 2, D_MODEL, 2 * D_FF), D_MODEL ** -0.5),
        "ffn_w_out": nrm(ks[11], (DEPTH, 2, D_FF, D_MODEL), D_FF ** -0.5),
        "w_in": nrm(ks[13], (DEPTH, D_MODEL, IN_WIDTH), D_MODEL ** -0.5),
        "ssd_conv_w": nrm(ks[14], (DEPTH, SSD_CONV, SSD_CONV_CH), SSD_CONV ** -0.5),
        "ssd_conv_b": nrm(ks[15], (DEPTH, SSD_CONV_CH), 0.02),
        "ssd_dt_bias": dt0 + jnp.log(-jnp.expm1(-dt0)),
        "ssd_a_log": jnp.log(jax.random.uniform(ks[16], (DEPTH, 2, SSD_HEADS), f32, 1.0, 16.0)),
        "ssd_d": 1.0 + nrm(ks[17], (DEPTH, SSD_HEADS), 0.1),
        "ssd_norm_gain": 1.0 + nrm(ks[18], (DEPTH, MIX_WIDTH), 0.05),
        "diff_lambda": nrm(ks[19], (DEPTH, 4, DIFF_HEAD_DIM), 0.1),
        "diff_norm_gain": 1.0 + nrm(ks[20], (DEPTH, 2 * DIFF_HEAD_DIM), 0.05),
        "pool_map": nrm(ks[21], (DEPTH, POOL_GROUPS, POOL_GROUP_WIDTH, POOL_GROUP_WIDTH), POOL_GROUP_WIDTH ** -0.5),
        "pool_scale": 1.0 + nrm(ks[22], (DEPTH, MIX_WIDTH), 0.05),
        "w_branch": nrm(ks[23], (DEPTH, N_BRANCH, MIX_WIDTH, D_MODEL), MIX_WIDTH ** -0.5),
        "w_out": nrm(ks[24], (DEPTH, D_MODEL, D_MODEL), D_MODEL ** -0.5),
        "final_gain": 1.0 + nrm(ks[25], (D_MODEL,), 0.05),
    }


def reference(x_prompt, x_sample, cache_k, cache_v, state_ssm, c, c_ctx, w_ada, b_ada, norm_gain,
              ffn_w_in, ffn_w_out, w_in, ssd_conv_w, ssd_conv_b, ssd_dt_bias, ssd_a_log, ssd_d,
              ssd_norm_gain, diff_lambda, diff_norm_gain, pool_map, pool_scale, w_branch, w_out,
              final_gain):
    p = {
        "w_ada": w_ada, "b_ada": b_ada, "norm_gain": norm_gain, "ffn_w_in": ffn_w_in,
        "ffn_w_out": ffn_w_out, "w_in": w_in, "ssd_conv_w": ssd_conv_w, "ssd_conv_b": ssd_conv_b,
        "ssd_dt_bias": ssd_dt_bias, "ssd_a_log": ssd_a_log, "ssd_d": ssd_d,
        "ssd_norm_gain": ssd_norm_gain, "diff_lambda": diff_lambda, "diff_norm_gain": diff_norm_gain,
        "pool_map": pool_map, "pool_scale": pool_scale, "w_branch": w_branch, "w_out": w_out,
    }
    xp = x_prompt
    cond_ctx = c_ctx[None, :]
    ks_list, vs_list, ss_list = [], [], []
    for l in range(DEPTH):
        xp, (k_l, v_l, s_l) = trunk_layer(xp, cond_ctx, l, p, None)
        ks_list.append(k_l)
        vs_list.append(v_l)
        ss_list.append(s_l)
    y_prompt = rmsnorm(xp, final_gain)
    new_cache_k = jnp.stack(ks_list, axis=1)
    new_cache_v = jnp.stack(vs_list, axis=1)
    new_state_ssm = jnp.stack(ss_list, axis=1)

    xs = x_sample
    for l in range(DEPTH):
        xs, _ = trunk_layer(xs, c, l, p, (cache_k[:, l], cache_v[:, l], state_ssm[:, l]))
    y_sample = rmsnorm(xs, final_gain)
    return (y_prompt, y_sample, new_cache_k, new_cache_v, new_state_ssm)
```

```python
import functools
import math

import jax
import jax.numpy as jnp
from jax import lax
from jax.experimental import pallas as pl
from jax.experimental.pallas import tpu as pltpu

F32 = jnp.float32
BF16 = jnp.bfloat16

D_MODEL = 1024
BATCH = 16
SEQ = 256
DEPTH = 4
DEC_BATCH = 8
DEC_SEQ = 1024
PAST_LEN = 512
GRID_W = 64
N_BRANCH = 3
MIX_WIDTH = D_MODEL
D_FF = 2816
SSD_HEAD_DIM = 64
SSD_HEADS = MIX_WIDTH // SSD_HEAD_DIM
SSD_GROUPS = 4
SSD_STATE = 64
SSD_CONV = 5
SSD_CHUNK = 128
SSD_CONV_CH = MIX_WIDTH + 2 * SSD_GROUPS * SSD_STATE
DIFF_HEAD_DIM = 64
DIFF_HEADS = MIX_WIDTH // (2 * DIFF_HEAD_DIM)
POOL_WINDOWS = (2, 4, 8, 16)
POOL_GROUPS = 4
POOL_GROUP_WIDTH = MIX_WIDTH // POOL_GROUPS
N_MOD = 9
ROPE_THETA = 10000.0
RMS_EPS = 1e-6

LANES = 128
V7X_SCOPED_VMEM_BYTES = 60000 * 1024

N_CTX_TOK = BATCH * SEQ
N_LAT_TOK = DEC_BATCH * DEC_SEQ
N_TOK = N_CTX_TOK + N_LAT_TOK
N_COND = 16
DT_PAD = LANES
XBCDT_W = SSD_CONV_CH + DT_PAD
HEADS_PER_GROUP = SSD_HEADS // SSD_GROUPS
GROUP_W = HEADS_PER_GROUP * SSD_HEAD_DIM


def _cparams(sem, vmem_bytes):
    return pltpu.CompilerParams(dimension_semantics=sem,
                                vmem_limit_bytes=int(min(vmem_bytes, V7X_SCOPED_VMEM_BYTES)))


def _resident(block_shape, index_map):
    return pl.BlockSpec(block_shape, index_map, pipeline_mode=pl.Buffered(1))


def _cond_row(i, tm):
    n_ctx_blk = N_CTX_TOK // tm
    per_seq = DEC_SEQ // tm
    return jnp.where(i < n_ctx_blk, 0, 1 + (i - n_ctx_blk) // per_seq)


def _silu(x):
    return x * jax.nn.sigmoid(x)


def _norm_mod(x, gain, shift, scale):
    y = x * lax.rsqrt(jnp.mean(x * x, axis=-1, keepdims=True) + RMS_EPS)
    return (y * gain) * (1.0 + scale) + shift


def _ada_kernel(c_ref, w_ref, b_ref, o_ref):
    s = _silu(c_ref[...]).astype(BF16)
    o_ref[...] = jnp.dot(s, w_ref[...].astype(BF16), preferred_element_type=F32) + b_ref[...]


def _ada(cond, w_ada, b_ada):
    tn = D_MODEL
    return pl.pallas_call(
        _ada_kernel,
        grid=(DEPTH, N_MOD * D_MODEL // tn),
        in_specs=[
            pl.BlockSpec((N_COND, D_MODEL), lambda l, j: (0, 0)),
            pl.BlockSpec((None, D_MODEL, tn), lambda l, j: (l, 0, j)),
            pl.BlockSpec((None, 1, tn), lambda l, j: (l, 0, j)),
        ],
        out_specs=pl.BlockSpec((None, N_COND, tn), lambda l, j: (l, 0, j)),
        out_shape=jax.ShapeDtypeStruct((DEPTH, N_COND, N_MOD * D_MODEL), F32),
        compiler_params=_cparams(("arbitrary", "arbitrary"), 24 << 20),
        name="ada",
    )(cond, w_ada, b_ada.reshape(DEPTH, 1, N_MOD * D_MODEL))


FFN_TM = 512
FFN_CHUNK = D_FF // 2


def _ffn_kernel(x_ref, mod_ref, g_ref, wi_ref, wo_ref, fg_ref, o_ref, act_scr, *, sub, final):
    x = x_ref[...]
    m0 = 3 * sub if sub == 0 else 6
    h = _norm_mod(x, g_ref[...], mod_ref[m0:m0 + 1, :], mod_ref[m0 + 1:m0 + 2, :]).astype(BF16)
    for c in range(D_FF // FFN_CHUNK):
        lo = c * FFN_CHUNK
        g = jnp.dot(h, wi_ref[:, lo:lo + FFN_CHUNK], preferred_element_type=F32)
        u = jnp.dot(h, wi_ref[:, D_FF + lo:D_FF + lo + FFN_CHUNK], preferred_element_type=F32)
        act_scr[:, lo:lo + FFN_CHUNK] = (_silu(g) * u).astype(BF16)
    y = jnp.dot(act_scr[...], wo_ref[...], preferred_element_type=F32)
    out = x + 0.5 * mod_ref[m0 + 2:m0 + 3, :] * y
    if final:
        out = out * lax.rsqrt(jnp.mean(out * out, axis=-1, keepdims=True) + RMS_EPS) * fg_ref[...]
    o_ref[...] = out


def _ffn(x, mod4, gain3, wi_bf, wo_bf, final_gain, l, sub, final):
    tm = FFN_TM
    grow = 0 if sub == 0 else 2
    return pl.pallas_call(
        functools.partial(_ffn_kernel, sub=sub, final=final),
        grid=(N_TOK // tm,),
        in_specs=[
            pl.BlockSpec((tm, D_MODEL), lambda i: (i, 0)),
            pl.BlockSpec((None, None, N_MOD, D_MODEL), lambda i: (l, _cond_row(i, tm), 0, 0)),
            pl.BlockSpec((None, None, 1, D_MODEL), lambda i: (l, grow, 0, 0)),
            _resident((None, None, D_MODEL, 2 * D_FF), lambda i: (l, sub, 0, 0)),
            _resident((None, None, D_FF, D_MODEL), lambda i: (l, sub, 0, 0)),
            pl.BlockSpec((1, D_MODEL), lambda i: (0, 0)),
        ],
        out_specs=pl.BlockSpec((tm, D_MODEL), lambda i: (i, 0)),
        out_shape=jax.ShapeDtypeStruct((N_TOK, D_MODEL), F32),
        scratch_shapes=[pltpu.VMEM((tm, D_FF), BF16)],
        compiler_params=_cparams(("arbitrary",), 56 << 20),
        name=f"ffn{sub}",
    )(x, mod4, gain3, wi_bf, wo_bf, final_gain)


PROJ_TM = 512


def _rope_tile(t, cos, sin, second):
    partner = jnp.where(second, pltpu.roll(t, 16, 1), pltpu.roll(t, LANES - 16, 1))
    return t * cos + partner * sin


def _proj_kernel(x_ref, mod_ref, g_ref, w_ref, *rest, rope_cols):
    if rope_cols:
        cos_ref, sin_ref, o_ref = rest
    else:
        (o_ref,) = rest
    h = _norm_mod(x_ref[...], g_ref[...], mod_ref[3:4, :], mod_ref[4:5, :]).astype(BF16)
    y = jnp.dot(h, w_ref[...], preferred_element_type=F32)
    if rope_cols:
        cos = cos_ref[...]
        sin = sin_ref[...]
        lane = lax.broadcasted_iota(jnp.int32, cos.shape, 1)
        second = (lane & 16) != 0
        for j in range(rope_cols // LANES):
            o_ref[:, j * LANES:(j + 1) * LANES] = _rope_tile(y[:, j * LANES:(j + 1) * LANES], cos, sin, second)
        o_ref[:, rope_cols:] = y[:, rope_cols:]
    else:
        o_ref[...] = y


def _proj(x, mod4, gain3, w_slab, l, rope=None):
    tm = PROJ_TM
    ncols = w_slab.shape[-1]
    in_specs = [
        pl.BlockSpec((tm, D_MODEL), lambda i: (i, 0)),
        pl.BlockSpec((None, None, N_MOD, D_MODEL), lambda i: (l, _cond_row(i, tm), 0, 0)),
        pl.BlockSpec((None, None, 1, D_MODEL), lambda i: (l, 1, 0, 0)),
        _resident((None, D_MODEL, ncols), lambda i: (l, 0, 0)),
    ]
    args = [x, mod4, gain3, w_slab]
    rope_cols = 0
    if rope is not None:
        n_ctx_blk = N_CTX_TOK // tm
        per_seq = DEC_SEQ // tm
        tab_map = lambda i: (jnp.where(i < n_ctx_blk, 0, 1 + (i - n_ctx_blk) % per_seq), 0)
        in_specs += [pl.BlockSpec((tm, LANES), tab_map), pl.BlockSpec((tm, LANES), tab_map)]
        args += list(rope)
        rope_cols = 2 * MIX_WIDTH
    return pl.pallas_call(
        functools.partial(_proj_kernel, rope_cols=rope_cols),
        grid=(N_TOK // tm,),
        in_specs=in_specs,
        out_specs=pl.BlockSpec((tm, ncols), lambda i: (i, 0)),
        out_shape=jax.ShapeDtypeStruct((N_TOK, ncols), F32),
        compiler_params=_cparams(("arbitrary",), 48 << 20),
        name=f"proj{ncols}",
    )(*args)


def _rope_tables(tm):
    rows = DEC_SEQ // GRID_W
    pos_row = jnp.repeat(jnp.arange(rows), GRID_W)
    pos_col = jnp.tile(jnp.arange(GRID_W), rows)
    n_freq = DIFF_HEAD_DIM // 4
    inv_freq = ROPE_THETA ** (-jnp.arange(n_freq, dtype=F32) / n_freq)

    def half(pos):
        ang = pos.astype(F32)[:, None] * inv_freq
        cos = jnp.cos(ang)
        sin = jnp.sin(ang)
        return jnp.concatenate([cos, cos], -1), jnp.concatenate([-sin, sin], -1)

    cr, sr = half(pos_row)
    cc, sc = half(pos_col)
    cos64 = jnp.concatenate([cr, cc], -1)
    sin64 = jnp.concatenate([sr, sc], -1)
    cos = jnp.concatenate([cos64, cos64], -1)
    sin = jnp.concatenate([sin64, sin64], -1)
    cos = jnp.concatenate([jnp.ones((tm, LANES), F32), cos], 0)
    sin = jnp.concatenate([jnp.zeros((tm, LANES), F32), sin], 0)
    return cos, sin


def _split3(x):
    hi = x.astype(BF16)
    r1 = x - hi.astype(F32)
    mid = r1.astype(BF16)
    lo = (r1 - mid.astype(F32)).astype(BF16)
    return hi, mid, lo


def _dot_sel_right(x, sel):
    hi, mid, lo = _split3(x)
    d = lambda a: jnp.dot(a, sel, preferred_element_type=F32)
    return d(hi) + d(mid) + d(lo)


def _dot_sel_left(sel, x):
    hi, mid, lo = _split3(x)
    d = lambda a: jnp.dot(sel, a, preferred_element_type=F32)
    return d(hi) + d(mid) + d(lo)


def _ssd_kernel(*refs, L, has_h0, emit_state):
    p_ref, cw_ref, cb_ref, dtb_ref, alog_ref, dskip_ref = refs[:6]
    k = 6
    h0_ref = None
    if has_h0:
        h0_ref = refs[k]
        k += 1
    y_ref = refs[k]
    k += 1
    st_ref = None
    if emit_state:
        st_ref = refs[k]
        k += 1
    xbc_scr, stf_scr, stb_scr = refs[k:]
    Q = SSD_CHUNK
    nc = L // Q
    H = SSD_HEADS
    HP = MIX_WIDTH

    rows = lax.broadcasted_iota(jnp.int32, (L, LANES), 0)
    for j in range(SSD_CONV_CH // LANES):
        cs = slice(j * LANES, (j + 1) * LANES)
        u = p_ref[:, cs]
        acc = cb_ref[:, cs] + cw_ref[SSD_CONV // 2:SSD_CONV // 2 + 1, cs] * u
        for tap in range(SSD_CONV):
            s = tap - SSD_CONV // 2
            if s == 0:
                continue
            if s < 0:
                sh = jnp.where(rows >= -s, pltpu.roll(u, -s, 0), 0.0)
            else:
                sh = jnp.where(rows < L - s, pltpu.roll(u, L - s, 0), 0.0)
            acc = acc + cw_ref[tap:tap + 1, cs] * sh
        xbc_scr[:, cs] = _silu(acc)

    if has_h0:
        stf_scr[...] = h0_ref[0]
        stb_scr[...] = h0_ref[1]
    else:
        stf_scr[...] = jnp.zeros_like(stf_scr)
        stb_scr[...] = jnp.zeros_like(stb_scr)

    ri = lax.broadcasted_iota(jnp.int32, (Q, Q), 0)
    ci = lax.broadcasted_iota(jnp.int32, (Q, Q), 1)
    lower = ri >= ci
    upper = ri <= ci
    tri_l = jnp.where(lower, 1.0, 0.0).astype(BF16)
    tri_u = jnp.where(upper, 1.0, 0.0).astype(BF16)
    ej = lax.broadcasted_iota(jnp.int32, (DT_PAD, 2 * HP), 0)
    ec = lax.broadcasted_iota(jnp.int32, (DT_PAD, 2 * HP), 1)
    expand = jnp.where(ec // SSD_HEAD_DIM == ej, 1.0, 0.0).astype(BF16)
    lane_q = lax.broadcasted_iota(jnp.int32, (Q, DT_PAD), 1)
    a_coef = -jnp.exp(alog_ref[...])
    dt_bias = dtb_ref[...]

    def chunk_terms(t0):
        raw = p_ref[pl.ds(t0, Q), SSD_CONV_CH:SSD_CONV_CH + DT_PAD] + dt_bias
        dt = jnp.maximum(raw, 0.0) + jnp.log1p(jnp.exp(-jnp.abs(raw)))
        a = dt * a_coef
        cum = jnp.where(lane_q < H, _dot_sel_left(tri_l, a), _dot_sel_left(tri_u, a))
        return dt, cum

    def fwd_chunk(c, carry):
        t0 = pl.multiple_of(c * Q, Q)
        dt, cum = chunk_terms(t0)
        cum_t = cum.T
        dt_t = dt.T
        cum_x = _dot_sel_right(cum, expand[:, :HP])
        dt_x = _dot_sel_right(dt, expand[:, :HP])
        xs = xbc_scr[pl.ds(t0, Q), 0:MIX_WIDTH]
        bm = xbc_scr[pl.ds(t0, Q), MIX_WIDTH:MIX_WIDTH + SSD_GROUPS * SSD_STATE]
        cm = xbc_scr[pl.ds(t0, Q), MIX_WIDTH + SSD_GROUPS * SSD_STATE:SSD_CONV_CH]
        xs_b = xs.astype(BF16)
        bm_b = bm.astype(BF16)
        cm_b = cm.astype(BF16)
        bm_t = bm.T.astype(BF16)
        last = cum_x[Q - 1:Q, :]
        e_in = jnp.exp(cum_x)
        w_out = jnp.exp(last - cum_x) * dt_x
        x_w = (xs * w_out).astype(BF16)
        dec = jnp.exp(last)
        for g in range(SSD_GROUPS):
            ns = slice(g * SSD_STATE, (g + 1) * SSD_STATE)
            gs = slice(g * GROUP_W, (g + 1) * GROUP_W)
            cb = lax.dot_general(cm_b[:, ns], bm_b[:, ns], (((1,), (1,)), ((), ())), preferred_element_type=F32)
            ys = []
            for hh in range(HEADS_PER_GROUP):
                h = g * HEADS_PER_GROUP + hh
                seg_f = jnp.where(lower, cum[:, h:h + 1] - cum_t[h:h + 1, :], -jnp.inf)
                seg_b = jnp.where(upper, cum[:, H + h:H + h + 1] - cum_t[H + h:H + h + 1, :], -jnp.inf)
                sc = cb * (jnp.exp(seg_f) * dt_t[h:h + 1, :] + jnp.exp(seg_b) * dt_t[H + h:H + h + 1, :])
                ys.append(jnp.dot(sc.astype(BF16), xs_b[:, h * SSD_HEAD_DIM:(h + 1) * SSD_HEAD_DIM],
                                  preferred_element_type=F32))
            y_diag = jnp.concatenate(ys, axis=1)
            st = stf_scr[:, gs]
            y_off = jnp.dot(cm_b[:, ns], st.astype(BF16), preferred_element_type=F32) * e_in[:, gs]
            y_ref[pl.ds(t0, Q), gs] = y_diag + y_off + dskip_ref[:, gs] * xs[:, gs]
            stf_scr[:, gs] = st * dec[:, gs] + jnp.dot(bm_t[ns, :], x_w[:, gs], preferred_element_type=F32)
        return carry

    def bwd_chunk(i, carry):
        c = nc - 1 - i
        t0 = pl.multiple_of(c * Q, Q)
        dt, cum = chunk_terms(t0)
        cum_x = _dot_sel_right(cum, expand[:, HP:])
        dt_x = _dot_sel_right(dt, expand[:, HP:])
        xs = xbc_scr[pl.ds(t0, Q), 0:MIX_WIDTH]
        bm = xbc_scr[pl.ds(t0, Q), MIX_WIDTH:MIX_WIDTH + SSD_GROUPS * SSD_STATE]
        cm_b = xbc_scr[pl.ds(t0, Q), MIX_WIDTH + SSD_GROUPS * SSD_STATE:SSD_CONV_CH].astype(BF16)
        bm_t = bm.T.astype(BF16)
        last = cum_x[0:1, :]
        e_in = jnp.exp(cum_x)
        w_out = jnp.exp(last - cum_x) * dt_x
        x_w = (xs * w_out).astype(BF16)
        dec = jnp.exp(last)
        for g in range(SSD_GROUPS):
            ns = slice(g * SSD_STATE, (g + 1) * SSD_STATE)
            gs = slice(g * GROUP_W, (g + 1) * GROUP_W)
            st = stb_scr[:, gs]
            y_off = jnp.dot(cm_b[:, ns], st.astype(BF16), preferred_element_type=F32) * e_in[:, gs]
            y_ref[pl.ds(t0, Q), gs] = y_ref[pl.ds(t0, Q), gs] + y_off
            stb_scr[:, gs] = st * dec[:, gs] + jnp.dot(bm_t[ns, :], x_w[:, gs], preferred_element_type=F32)
        return carry

    lax.fori_loop(0, nc, fwd_chunk, 0)
    lax.fori_loop(0, nc, bwd_chunk, 0)
    if emit_state:
        st_ref[0] = stf_scr[...]
        st_ref[1] = stb_scr[...]


def _ssd(xbcdt, row_off, n_seq, L, conv_w, conv_b, dt_bias, a_log, d_skip, l, h0_t, emit_state):
    blk_off = row_off // L
    has_h0 = h0_t is not None
    in_specs = [
        pl.BlockSpec((L, XBCDT_W), lambda b: (blk_off + b, 0)),
        pl.BlockSpec((None, SSD_CONV, SSD_CONV_CH), lambda b: (l, 0, 0)),
        pl.BlockSpec((None, 1, SSD_CONV_CH), lambda b: (l, 0, 0)),
        pl.BlockSpec((None, 1, DT_PAD), lambda b: (l, 0, 0)),
        pl.BlockSpec((None, 1, DT_PAD), lambda b: (l, 0, 0)),
        pl.BlockSpec((None, 1, MIX_WIDTH), lambda b: (l, 0, 0)),
    ]
    args = [xbcdt, conv_w, conv_b, dt_bias, a_log, d_skip]
    if has_h0:
        in_specs.append(pl.BlockSpec((None, None, 2, SSD_STATE, MIX_WIDTH), lambda b: (b, l, 0, 0, 0)))
        args.append(h0_t)
    out_specs = [pl.BlockSpec((L, MIX_WIDTH), lambda b: (b, 0))]
    out_shape = [jax.ShapeDtypeStruct((n_seq * L, MIX_WIDTH), F32)]
    if emit_state:
        out_specs.append(pl.BlockSpec((None, 2, SSD_STATE, MIX_WIDTH), lambda b: (b, 0, 0, 0)))
        out_shape.append(jax.ShapeDtypeStruct((n_seq, 2, SSD_STATE, MIX_WIDTH), F32))
    res = pl.pallas_call(
        functools.partial(_ssd_kernel, L=L, has_h0=has_h0, emit_state=emit_state),
        grid=(n_seq,),
        in_specs=in_specs,
        out_specs=out_specs,
        out_shape=out_shape,
        scratch_shapes=[pltpu.VMEM((L, SSD_CONV_CH), F32),
                        pltpu.VMEM((SSD_STATE, MIX_WIDTH), F32),
                        pltpu.VMEM((SSD_STATE, MIX_WIDTH), F32)],
        compiler_params=_cparams(("arbitrary",), 48 << 20),
        name=f"ssd{L}",
    )(*args)
    return res


def _attn_kernel(*refs, has_cache, lam_init):
    lam_ref, q_ref, k_ref, v_ref = refs[:4]
    k0 = 4
    if has_cache:
        kc_ref, vc_ref = refs[4:6]
        k0 = 6
    gain_ref, o_ref = refs[k0:]
    tq = q_ref.shape[0]
    d = DIFF_HEAD_DIM
    lp = lam_ref[...]
    s1 = jnp.sum(lp[0:1, :] * lp[1:2, :], axis=-1, keepdims=True)
    s2 = jnp.sum(lp[2:3, :] * lp[3:4, :], axis=-1, keepdims=True)
    lam = jnp.exp(s1) - jnp.exp(s2) + lam_init

    q = q_ref[...] * (d ** -0.5)
    lane = lax.broadcasted_iota(jnp.int32, q.shape, 1)
    first = lane < d
    qs = jnp.concatenate([jnp.where(first, q, 0.0), jnp.where(first, 0.0, q)], axis=0).astype(BF16)
    nt = (((1,), (1,)), ((), ()))
    keys = [k_ref[...].astype(BF16)]
    vals = [v_ref[...].astype(BF16)]
    if has_cache:
        keys.insert(0, kc_ref[...].astype(BF16))
        vals.insert(0, vc_ref[...].astype(BF16))
    ss = [lax.dot_general(qs, kk, nt, preferred_element_type=F32) for kk in keys]
    m = functools.reduce(jnp.maximum, [jnp.max(s, axis=-1, keepdims=True) for s in ss])
    es = [jnp.exp(s - m) for s in ss]
    den = functools.reduce(lambda a, b: a + b, [jnp.sum(e, axis=-1, keepdims=True) for e in es])
    r = 1.0 / den
    c1 = r[:tq]
    c2 = r[tq:] * lam
    o = None
    for e, vv in zip(es, vals):
        a = (e[:tq] * c1 - e[tq:] * c2).astype(BF16)
        t = jnp.dot(a, vv, preferred_element_type=F32)
        o = t if o is None else o + t
    o = o * lax.rsqrt(jnp.mean(o * o, axis=-1, keepdims=True) + RMS_EPS)
    o_ref[...] = o * gain_ref[...] * (1.0 - lam_init)


def _attn(qkv, row_off, n_seq, L, tq, diff_lambda, diff_gain, l, cache):
    nq = L // tq
    hw = 2 * DIFF_HEAD_DIM
    q_off = row_off // tq
    s_off = row_off // L
    lam_init = 0.8 - 0.6 * math.exp(-0.3 * l)
    in_specs = [
        pl.BlockSpec((None, 4, DIFF_HEAD_DIM), lambda b, h, i: (l, 0, 0)),
        pl.BlockSpec((tq, hw), lambda b, h, i: (q_off + b * nq + i, h)),
        pl.BlockSpec((L, hw), lambda b, h, i: (s_off + b, DIFF_HEADS + h)),
        pl.BlockSpec((L, hw), lambda b, h, i: (s_off + b, 2 * DIFF_HEADS + h)),
    ]
    args = [diff_lambda, qkv, qkv, qkv]
    if cache is not None:
        in_specs += [pl.BlockSpec((None, None, PAST_LEN, hw), lambda b, h, i: (b, l, 0, h))] * 2
        args += list(cache)
    in_specs.append(pl.BlockSpec((None, 1, hw), lambda b, h, i: (l, 0, 0)))
    args.append(diff_gain)
    return pl.pallas_call(
        functools.partial(_attn_kernel, has_cache=cache is not None, lam_init=lam_init),
        grid=(n_seq, DIFF_HEADS, nq),
        in_specs=in_specs,
        out_specs=pl.BlockSpec((tq, hw), lambda b, h, i: (b * nq + i, h)),
        out_shape=jax.ShapeDtypeStruct((n_seq * L, MIX_WIDTH), F32),
        compiler_params=_cparams(("arbitrary", "arbitrary", "arbitrary"), 40 << 20),
        name=f"attn{L}",
    )(*args)


def _pool_kernel(u_ref, map_ref, scale_ref, o_ref, *, L):
    gw = POOL_GROUP_WIDTH
    rows = lax.broadcasted_iota(jnp.int32, (L, gw), 0)

    def down(x, k):
        return jnp.where(rows >= k, pltpu.roll(x, k, 0), 0.0)

    def up(x, k):
        return jnp.where(rows < L - k, pltpu.roll(x, L - k, 0), 0.0)

    t = rows.astype(F32)
    for g, win in enumerate(POOL_WINDOWS):
        cs = slice(g * gw, (g + 1) * gw)
        u = u_ref[:, cs]
        half = win // 2
        back = u
        fwd = u
        m = 1
        while m < half:
            back = back + down(back, m)
            fwd = fwd + up(fwd, m)
            m *= 2
        window_sum = down(back, 1) + fwd
        count = jnp.minimum(t + half, float(L)) - jnp.maximum(t - half, 0.0)
        pooled = window_sum / count - u
        mixed = jnp.dot(pooled.astype(BF16), map_ref[g], preferred_element_type=F32)
        o_ref[:, cs] = mixed * scale_ref[:, cs]


def _pool(u_all, row_off, n_seq, L, pool_map_bf, pool_scale, l):
    blk_off = row_off // L
    return pl.pallas_call(
        functools.partial(_pool_kernel, L=L),
        grid=(n_seq,),
        in_specs=[
            pl.BlockSpec((L, MIX_WIDTH), lambda b: (blk_off + b, 0)),
            pl.BlockSpec((None, POOL_GROUPS, POOL_GROUP_WIDTH, POOL_GROUP_WIDTH), lambda b: (l, 0, 0, 0)),
            pl.BlockSpec((None, 1, MIX_WIDTH), lambda b: (l, 0, 0)),
        ],
        out_specs=pl.BlockSpec((L, MIX_WIDTH), lambda b: (b, 0)),
        out_shape=jax.ShapeDtypeStruct((n_seq * L, MIX_WIDTH), F32),
        compiler_params=_cparams(("arbitrary",), 40 << 20),
        name=f"pool{L}",
    )(u_all, pool_map_bf, pool_scale)


MERGE_TM = 256


def _merge_kernel(x_ref, mod_ref, ya_ref, yb_ref, yc_ref, zg_ref, ng_ref, wb_ref, wo_ref, o_ref):
    W = MIX_WIDTH
    ya = ya_ref[...] * _silu(zg_ref[:, 0:W])
    ya = ya * lax.rsqrt(jnp.mean(ya * ya, axis=-1, keepdims=True) + RMS_EPS) * ng_ref[...]
    merged = None
    for n, br in enumerate((ya, yb_ref[...], yc_ref[...])):
        pb = jnp.dot(br.astype(BF16), wb_ref[n], preferred_element_type=F32)
        term = jax.nn.sigmoid(zg_ref[:, (n + 1) * W:(n + 2) * W]) * pb
        merged = term if merged is None else merged + term
    out = jnp.dot(merged.astype(BF16), wo_ref[...], preferred_element_type=F32)
    o_ref[...] = x_ref[...] + mod_ref[5:6, :] * out


def _merge(x, mod4, ya, yb, yc, zg, ssd_gain, wb_bf, wo_bf, l):
    tm = MERGE_TM
    row = lambda i: (i, 0)
    return pl.pallas_call(
        _merge_kernel,
        grid=(N_TOK // tm,),
        in_specs=[
            pl.BlockSpec((tm, D_MODEL), row),
            pl.BlockSpec((None, None, N_MOD, D_MODEL), lambda i: (l, _cond_row(i, tm), 0, 0)),
            pl.BlockSpec((tm, MIX_WIDTH), row),
            pl.BlockSpec((tm, MIX_WIDTH), row),
            pl.BlockSpec((tm, MIX_WIDTH), row),
            pl.BlockSpec((tm, (1 + N_BRANCH) * MIX_WIDTH), row),
            pl.BlockSpec((None, 1, MIX_WIDTH), lambda i: (l, 0, 0)),
            _resident((None, N_BRANCH, MIX_WIDTH, D_MODEL), lambda i: (l, 0, 0, 0)),
            _resident((None, MIX_WIDTH, D_MODEL), lambda i: (l, 0, 0)),
        ],
        out_specs=pl.BlockSpec((tm, D_MODEL), row),
        out_shape=jax.ShapeDtypeStruct((N_TOK, D_MODEL), F32),
        compiler_params=_cparams(("arbitrary",), 48 << 20),
        name="merge",
    )(x, mod4, ya, yb, yc, zg, ssd_gain, wb_bf, wo_bf)


def kernel(x_prompt, x_sample, cache_k, cache_v, state_ssm, c, c_ctx, w_ada, b_ada, norm_gain, ffn_w_in, ffn_w_out,
           w_in, ssd_conv_w, ssd_conv_b, ssd_dt_bias, ssd_a_log, ssd_d, ssd_norm_gain, diff_lambda, diff_norm_gain,
           pool_map, pool_scale, w_branch, w_out, final_gain):
    x = jnp.concatenate([x_prompt.reshape(N_CTX_TOK, D_MODEL), x_sample.reshape(N_LAT_TOK, D_MODEL)], axis=0)
    cond = jnp.concatenate([c_ctx[None, :], c, jnp.zeros((N_COND - 1 - DEC_BATCH, D_MODEL), F32)], axis=0)
    mod4 = _ada(cond, w_ada, b_ada).reshape(DEPTH, N_COND, N_MOD, D_MODEL)

    gain3 = norm_gain.reshape(DEPTH, 3, 1, D_MODEL)
    wi_bf = ffn_w_in.astype(BF16)
    wo_bf = ffn_w_out.astype(BF16)
    o_xbc = MIX_WIDTH
    o_dt = o_xbc + SSD_CONV_CH
    o_q = o_dt + 2 * SSD_HEADS
    o_pool = o_q + 3 * MIX_WIDTH
    o_gate = o_pool + MIX_WIDTH
    w_zg = jnp.concatenate([w_in[:, :, :o_xbc], w_in[:, :, o_gate:]], axis=-1).astype(BF16)
    w_xbcdt = jnp.pad(w_in[:, :, o_xbc:o_q], ((0, 0), (0, 0), (0, DT_PAD - 2 * SSD_HEADS))).astype(BF16)
    w_qkv = w_in[:, :, o_q:o_pool].astype(BF16)
    w_pool = w_in[:, :, o_pool:o_gate].astype(BF16)
    wb_bf = w_branch.astype(BF16)
    wout_bf = w_out.astype(BF16)
    map_bf = pool_map.astype(BF16)
    conv_b = ssd_conv_b.reshape(DEPTH, 1, SSD_CONV_CH)
    pad_dt = lambda a: jnp.pad(a.reshape(DEPTH, 1, 2 * SSD_HEADS), ((0, 0), (0, 0), (0, DT_PAD - 2 * SSD_HEADS)))
    dt_bias = pad_dt(ssd_dt_bias)
    a_log = pad_dt(ssd_a_log)
    d_skip = jnp.repeat(ssd_d, SSD_HEAD_DIM, axis=-1).reshape(DEPTH, 1, MIX_WIDTH)
    ssd_gain = ssd_norm_gain.reshape(DEPTH, 1, MIX_WIDTH)
    diff_gain = diff_norm_gain.reshape(DEPTH, 1, 2 * DIFF_HEAD_DIM)
    p_scale = pool_scale.reshape(DEPTH, 1, MIX_WIDTH)
    fgain = final_gain.reshape(1, D_MODEL)
    rope = _rope_tables(PROJ_TM)
    ck = cache_k.reshape(DEC_BATCH, DEPTH, PAST_LEN, MIX_WIDTH)
    cv = cache_v.reshape(DEC_BATCH, DEPTH, PAST_LEN, MIX_WIDTH)
    h0_t = state_ssm.reshape(DEC_BATCH, DEPTH, 2, MIX_WIDTH, SSD_STATE).swapaxes(-1, -2)

    ks, vs, sts = [], [], []
    for l in range(DEPTH):
        x = _ffn(x, mod4, gain3, wi_bf, wo_bf, fgain, l, 0, False)
        zg = _proj(x, mod4, gain3, w_zg, l)
        xbcdt = _proj(x, mod4, gain3, w_xbcdt, l)
        qkv = _proj(x, mod4, gain3, w_qkv, l, rope=rope)
        up = _proj(x, mod4, gain3, w_pool, l)

        ya_c, st_c = _ssd(xbcdt, 0, BATCH, SEQ, ssd_conv_w, conv_b, dt_bias, a_log, d_skip, l, None, True)
        (ya_l,) = _ssd(xbcdt, N_CTX_TOK, DEC_BATCH, DEC_SEQ, ssd_conv_w, conv_b, dt_bias, a_log, d_skip, l,
                       h0_t, False)
        yb_c = _attn(qkv, 0, BATCH, SEQ, SEQ, diff_lambda, diff_gain, l, None)
        yb_l = _attn(qkv, N_CTX_TOK, DEC_BATCH, DEC_SEQ, 256, diff_lambda, diff_gain, l, (ck, cv))
        yc_c = _pool(up, 0, BATCH, SEQ, map_bf, p_scale, l)
        yc_l = _pool(up, N_CTX_TOK, DEC_BATCH, DEC_SEQ, map_bf, p_scale, l)
        ya = jnp.concatenate([ya_c, ya_l], axis=0)
        yb = jnp.concatenate([yb_c, yb_l], axis=0)
        yc = jnp.concatenate([yc_c, yc_l], axis=0)

        x = _merge(x, mod4, ya, yb, yc, zg, ssd_gain, wb_bf, wout_bf, l)
        x = _ffn(x, mod4, gain3, wi_bf, wo_bf, fgain, l, 1, l == DEPTH - 1)

        ks.append(qkv[:N_CTX_TOK, MIX_WIDTH:2 * MIX_WIDTH].reshape(BATCH, SEQ, DIFF_HEADS, 2 * DIFF_HEAD_DIM))
        vs.append(qkv[:N_CTX_TOK, 2 * MIX_WIDTH:].reshape(BATCH, SEQ, DIFF_HEADS, 2 * DIFF_HEAD_DIM))
        sts.append(st_c.swapaxes(-1, -2).reshape(BATCH, 2, SSD_HEADS, SSD_HEAD_DIM, SSD_STATE))

    y_prompt = x[:N_CTX_TOK].reshape(BATCH, SEQ, D_MODEL)
    y_sample = x[N_CTX_TOK:].reshape(DEC_BATCH, DEC_SEQ, D_MODEL)
    return (y_prompt, y_sample, jnp.stack(ks, axis=1), jnp.stack(vs, axis=1), jnp.stack(sts, axis=1))
```
